```python
import jax, jax.numpy as jnp
from jax import lax
import numpy as np

D_MODEL = 1024
BATCH = 4
SEQ = 4096
DEPTH = 1

GLA_HEADS = 4
GLA_QK = D_MODEL // 2
GLA_V = D_MODEL
GLA_DK = GLA_QK // GLA_HEADS
GLA_DV = GLA_V // GLA_HEADS
GLA_LOWRANK = 16
GLA_LOGIT_NORM = 16.0
HGRN_EXPAND = 128
HGRN_HEADS = D_MODEL // HGRN_EXPAND
HGRN_K = D_MODEL
HGRN_V = D_MODEL
HGRN_DV = HGRN_V // HGRN_HEADS
D_FF = ((8 * D_MODEL // 3 + 255) // 256) * 256
CONV_W = 3
CHUNK = 64
EPS = 1e-6
IN_SIZES = (GLA_QK, GLA_QK, GLA_V, GLA_V, GLA_LOWRANK,
            HGRN_K, HGRN_K, HGRN_V, HGRN_V, D_MODEL, D_MODEL)
D_IN = sum(IN_SIZES)

kernel_name = "hybrid_gla_hgrn2_convffn_adaln"


def rmsnorm(x, w):
    x32 = x.astype(jnp.float32)
    return x32 * lax.rsqrt(jnp.mean(x32 * x32, axis=-1, keepdims=True) + EPS) * w.astype(jnp.float32)


def chunk_gated_linear_attn(q, k, v, log_g, scale):
    B, T, H, dk = q.shape
    dv = v.shape[-1]
    N = T // CHUNK

    def blk(t):
        return t.astype(jnp.float32).reshape(B, N, CHUNK, H, t.shape[-1]).transpose(0, 3, 1, 2, 4)

    q, k, v, log_g = blk(q) * scale, blk(k), blk(v), blk(log_g)
    b = jnp.cumsum(log_g, axis=3)
    b_ref = b[..., CHUNK // 2 - 1:CHUNK // 2, :]
    b_last = b[..., -1:, :]
    causal = jnp.tril(jnp.ones((CHUNK, CHUNK), dtype=bool))
    scores = jnp.einsum('bhnik,bhnjk->bhnij', q * jnp.exp(b - b_ref), k * jnp.exp(b_ref - b))
    scores = jnp.where(causal, scores, 0.0)
    o_intra = jnp.einsum('bhnij,bhnjv->bhniv', scores, v)
    chunk_state = jnp.einsum('bhnjk,bhnjv->bhnkv', k * jnp.exp(b_last - b), v)
    chunk_decay = jnp.exp(b_last[..., 0, :])

    def step(S, inp):
        d, u = inp
        return d[..., None] * S + u, S

    _, S_prev = lax.scan(step, jnp.zeros((B, H, dk, dv), jnp.float32),
                         (jnp.moveaxis(chunk_decay, 2, 0), jnp.moveaxis(chunk_state, 2, 0)))
    S_prev = jnp.moveaxis(S_prev, 0, 2)
    o_inter = jnp.einsum('bhnik,bhnkv->bhniv', q * jnp.exp(b), S_prev)
    return (o_intra + o_inter).transpose(0, 2, 3, 1, 4).reshape(B, T, H, dv)


def token_mixer(h, w_in, w_gk2, b_gk2, gla_norm_w, lb, hgrn_norm_w, w_out):
    B, T, _ = h.shape
    proj = h @ w_in
    idx = np.cumsum(np.array(IN_SIZES))[:-1].tolist()
    gq, gk, gv, gg, glr, hq, hf, hi, hg, ma, mb = jnp.split(proj, idx, axis=-1)

    def heads(t, n):
        return t.reshape(B, T, n, t.shape[-1] // n)

    log_alpha = jax.nn.log_sigmoid((glr @ w_gk2 + b_gk2).astype(jnp.float32)) / GLA_LOGIT_NORM
    o_a = chunk_gated_linear_attn(heads(gq, GLA_HEADS), heads(gk, GLA_HEADS), heads(gv, GLA_HEADS),
                                  heads(log_alpha, GLA_HEADS), GLA_DK ** -0.5)
    y_a = (rmsnorm(o_a, gla_norm_w) * jax.nn.silu(heads(gg, GLA_HEADS))).reshape(B, T, GLA_V)

    sf = jax.nn.sigmoid(hf.astype(jnp.float32))
    f = lb + (1.0 - lb) * sf
    key = (1.0 - lb) * jax.nn.sigmoid(-hf.astype(jnp.float32))
    o_b = chunk_gated_linear_attn(heads(jax.nn.silu(hq), HGRN_HEADS), heads(key, HGRN_HEADS),
                                  heads(hi, HGRN_HEADS), heads(jnp.log(f), HGRN_HEADS), HGRN_EXPAND ** -0.5)
    y_b = (rmsnorm(o_b, hgrn_norm_w) * jax.nn.silu(heads(hg, HGRN_HEADS))).reshape(B, T, HGRN_V)

    merged = jax.nn.sigmoid(ma) * y_a + jax.nn.sigmoid(mb) * y_b
    return merged @ w_out


def conv_ffn(h, w_up, conv_w, conv_b, w_down):
    u = h @ w_up
    u = lax.conv_general_dilated(u, conv_w[:, None, :].astype(u.dtype), window_strides=(1,),
                                 padding=[(CONV_W - 1, 0)],
                                 dimension_numbers=('NWC', 'WIO', 'NWC'),
                                 feature_group_count=2 * D_FF) + conv_b
    gate, up = jnp.split(u, 2, axis=-1)
    return (jax.nn.silu(gate) * up) @ w_down


def setup_inputs(seed: int = 0) -> dict:
    key = jax.random.key(seed)
    ks = jax.random.split(key, 20)

    def nrm(k, shape, scale):
        return jax.random.normal(k, shape, jnp.float32) * scale

    return {
        "x": nrm(ks[0], (BATCH, SEQ, D_MODEL), 1.0),
        "c": nrm(ks[1], (BATCH, D_MODEL), 1.0),
        "w_ada": nrm(ks[2], (DEPTH, D_MODEL, 6 * D_MODEL), 0.5 * D_MODEL ** -0.5),
        "b_ada": nrm(ks[3], (DEPTH, 6 * D_MODEL), 0.02),
        "norm_mix_w": 1.0 + nrm(ks[4], (DEPTH, D_MODEL), 0.02),
        "w_in": nrm(ks[5], (DEPTH, D_MODEL, D_IN), D_MODEL ** -0.5),
        "w_gk2": nrm(ks[6], (DEPTH, GLA_LOWRANK, GLA_QK), GLA_LOWRANK ** -0.5),
        "b_gk2": nrm(ks[7], (DEPTH, GLA_QK), 0.1),
        "gla_norm_w": 1.0 + nrm(ks[8], (DEPTH, GLA_DV), 0.02),
        "hgrn_lb": nrm(ks[9], (DEPTH + 1, HGRN_K), 0.1),
        "hgrn_norm_w": 1.0 + nrm(ks[10], (DEPTH, HGRN_DV), 0.02),
        "w_out": nrm(ks[11], (DEPTH, D_MODEL, D_MODEL), D_MODEL ** -0.5),
        "norm_ffn_w": 1.0 + nrm(ks[12], (DEPTH, D_MODEL), 0.02),
        "w_up": nrm(ks[13], (DEPTH, D_MODEL, 2 * D_FF), D_MODEL ** -0.5),
        "conv_w": nrm(ks[14], (DEPTH, CONV_W, 2 * D_FF), CONV_W ** -0.5),
        "conv_b": nrm(ks[15], (DEPTH, 2 * D_FF), 0.02),
        "w_down": nrm(ks[16], (DEPTH, D_FF, D_MODEL), D_FF ** -0.5),
        "final_norm_w": 1.0 + nrm(ks[17], (D_MODEL,), 0.02),
    }


def reference(x, c, w_ada, b_ada, norm_mix_w, w_in, w_gk2, b_gk2, gla_norm_w, hgrn_lb,
              hgrn_norm_w, w_out, norm_ffn_w, w_up, conv_w, conv_b, w_down, final_norm_w):
    out_dtype = x.dtype
    lb_all = jnp.cumsum(jax.nn.softmax(hgrn_lb.astype(jnp.float32), axis=0), axis=0)
    c_act = jax.nn.silu(c.astype(jnp.float32))
    h_res = x
    for l in range(DEPTH):
        ada = c_act @ w_ada[l] + b_ada[l]
        sh_a, sc_a, g_a, sh_f, sc_f, g_f = [t[:, None, :] for t in jnp.split(ada, 6, axis=-1)]
        h = rmsnorm(h_res, norm_mix_w[l]) * (1.0 + sc_a) + sh_a
        h_res = h_res + g_a * token_mixer(h, w_in[l], w_gk2[l], b_gk2[l], gla_norm_w[l],
                                          lb_all[l], hgrn_norm_w[l], w_out[l])
        h = rmsnorm(h_res, norm_ffn_w[l]) * (1.0 + sc_f) + sh_f
        h_res = h_res + g_f * conv_ffn(h, w_up[l], conv_w[l], conv_b[l], w_down[l])
    return rmsnorm(h_res, final_norm_w).astype(out_dtype)
```

```python
import functools

import jax
import jax.numpy as jnp
from jax import lax
from jax.experimental import pallas as pl
from jax.experimental.pallas import tpu as pltpu

F32 = jnp.float32
BF16 = jnp.bfloat16

D_MODEL = 1024
GLA_HEADS = 4
GLA_QK = 512
GLA_V = 1024
GLA_DK = GLA_QK // GLA_HEADS
GLA_DV = GLA_V // GLA_HEADS
GLA_LOWRANK = 16
GLA_LOGIT_NORM = 16.0
HGRN_HEADS = 8
HGRN_K = 1024
HGRN_V = 1024
HGRN_DK = HGRN_K // HGRN_HEADS
HGRN_DV = HGRN_V // HGRN_HEADS
D_FF = 2816
CONV_W = 3
CHUNK = 64
EPS = 1e-6

LANES = 128
SUBLANES = 8
VMEM_LIMIT_BYTES = 56 * 1024 * 1024

OFF_GQ = 0
OFF_GK = OFF_GQ + GLA_QK
OFF_GV = OFF_GK + GLA_QK
OFF_GG = OFF_GV + GLA_V
OFF_HQ = OFF_GG + GLA_V
OFF_HF = OFF_HQ + HGRN_K
OFF_HI = OFF_HF + HGRN_K
OFF_HG = OFF_HI + HGRN_V
OFF_MA = OFF_HG + HGRN_V
OFF_MB = OFF_MA + D_MODEL
OFF_LR = OFF_MB + D_MODEL
D_IN_PAD = OFF_LR + LANES

TM_MIX = 256
TM_FFN = 512
FFN_TILE = 256


def _dot(a, b):
    return jnp.dot(a, b, preferred_element_type=F32)


def _sigmoid(x):
    e = jnp.exp(-jnp.abs(x))
    r = 1.0 / (1.0 + e)
    return jnp.where(x >= 0, r, e * r)


def _silu(x):
    return x * _sigmoid(x)


def _log_sigmoid(x):
    return jnp.minimum(x, 0.0) - jnp.log1p(jnp.exp(-jnp.abs(x)))


def _modulated_rmsnorm(x, w, scale, shift):
    ms = jnp.mean(x * x, axis=-1, keepdims=True)
    return x * lax.rsqrt(ms + EPS) * w * (1.0 + scale) + shift


def _ada_kernel(c_ref, w_ref, b_ref, o_ref):
    ca = _silu(c_ref[...])
    o_ref[0] = jnp.dot(ca, w_ref[...], precision=lax.Precision.HIGHEST,
                       preferred_element_type=F32) + b_ref[...]


def _ada_call(c_pad, w_ada, b_ada):
    n_tiles = w_ada.shape[1] // D_MODEL
    rows = c_pad.shape[0]
    return pl.pallas_call(
        _ada_kernel,
        grid=(n_tiles,),
        in_specs=[
            pl.BlockSpec((rows, D_MODEL), lambda j: (0, 0)),
            pl.BlockSpec((D_MODEL, D_MODEL), lambda j: (0, j)),
            pl.BlockSpec((1, D_MODEL), lambda j: (0, j)),
        ],
        out_specs=pl.BlockSpec((1, rows, D_MODEL), lambda j: (j, 0, 0)),
        out_shape=jax.ShapeDtypeStruct((n_tiles, rows, D_MODEL), F32),
        name="ada",
    )(c_pad, w_ada, b_ada)


def _chunk_cumsum(tril, x):
    hi = x.astype(BF16)
    lo = (x - hi.astype(F32)).astype(BF16)
    return _dot(tril, hi) + _dot(tril, lo)


def _gla_operands(q, k, log_g, qt_ref, kt_ref, qh_ref, kh_ref, dec_ref, tril):
    tm, c = q.shape
    n = tm // CHUNK
    b = _chunk_cumsum(tril, log_g)
    b3 = b.reshape(n, CHUNK, c)
    b_ref = b3[:, CHUNK // 2 - 1:CHUNK // 2, :]
    b_last = b3[:, CHUNK - 1:CHUNK, :]
    q3 = q.reshape(n, CHUNK, c)
    k3 = k.reshape(n, CHUNK, c)
    qt_ref[...] = (q3 * jnp.exp(b3 - b_ref)).reshape(tm, c).astype(BF16)
    kt_ref[...] = (k3 * jnp.exp(b_ref - b3)).reshape(tm, c).astype(BF16)
    qh_ref[...] = (q3 * jnp.exp(b3)).reshape(tm, c).astype(BF16)
    kh_ref[...] = (k3 * jnp.exp(b_last - b3)).reshape(tm, c).astype(BF16)
    dec_ref[...] = jnp.exp(jnp.transpose(b))


def _recurrence(qt_ref, kt_ref, qh_ref, kh_ref, v_ref, dec_ref, s_ref, o_ref,
                heads, dk, dv, tm, causal):
    n = tm // CHUNK
    for h in range(heads):
        ks = slice(h * dk, (h + 1) * dk)
        vs = slice(h * dv, (h + 1) * dv)
        s = s_ref[h]
        for i in range(n):
            rows = slice(i * CHUNK, (i + 1) * CHUNK)
            scores = lax.dot_general(qt_ref[rows, ks], kt_ref[rows, ks],
                                     (((1,), (1,)), ((), ())),
                                     preferred_element_type=F32)
            p = jnp.where(causal, scores, 0.0).astype(BF16)
            v = v_ref[rows, vs]
            lhs = jnp.concatenate([qh_ref[rows, ks], p], axis=1)
            rhs = jnp.concatenate([s.astype(BF16), v], axis=0)
            o_ref[rows, vs] = _dot(lhs, rhs)
            upd = lax.dot_general(kh_ref[rows, ks], v,
                                  (((0,), (0,)), ((), ())),
                                  preferred_element_type=F32)
            col = i * CHUNK + CHUNK - 1
            s = dec_ref[ks, col:col + 1] * s + upd
        s_ref[h] = s


def _mixer_kernel(x_ref, ada_ref, nw_ref, win_ref, wgk2_ref, bgk2_ref, gnw_ref, lbp_ref,
                  hnw_ref, wout_ref, o_ref,
                  h_scr, gqt, gkt, gqh, gkh, gv, gdec, hqt, hkt, hqh, hkh, hv, hdec,
                  og, oh, mrg, sg, sh):
    tm = TM_MIX

    @pl.when(pl.program_id(1) == 0)
    def _():
        sg[...] = jnp.zeros_like(sg)
        sh[...] = jnp.zeros_like(sh)

    x = x_ref[0]
    shift_a = ada_ref[0, 0:1, :]
    scale_a = ada_ref[0, 1:2, :]
    gate_a = ada_ref[0, 2:3, :]
    h_scr[...] = _modulated_rmsnorm(x, nw_ref[...], scale_a, shift_a).astype(BF16)

    def proj(off, width):
        return _dot(h_scr[...], win_ref[:, off:off + width])

    row = lax.broadcasted_iota(jnp.int32, (tm, tm), 0)
    col = lax.broadcasted_iota(jnp.int32, (tm, tm), 1)
    tril = ((row >= col) & (row // CHUNK == col // CHUNK)).astype(BF16)
    crow = lax.broadcasted_iota(jnp.int32, (CHUNK, CHUNK), 0)
    ccol = lax.broadcasted_iota(jnp.int32, (CHUNK, CHUNK), 1)
    causal = crow >= ccol

    glr = proj(OFF_LR, LANES).astype(BF16)
    z = _dot(glr, wgk2_ref[...]) + bgk2_ref[...]
    log_alpha = _log_sigmoid(z) * (1.0 / GLA_LOGIT_NORM)
    _gla_operands(proj(OFF_GQ, GLA_QK) * (GLA_DK ** -0.5), proj(OFF_GK, GLA_QK), log_alpha,
                  gqt, gkt, gqh, gkh, gdec, tril)
    gv[...] = proj(OFF_GV, GLA_V).astype(BF16)
    _recurrence(gqt, gkt, gqh, gkh, gv, gdec, sg, og, GLA_HEADS, GLA_DK, GLA_DV, tm, causal)

    lmax = jnp.maximum(lbp_ref[0:1, :], lbp_ref[1:2, :])
    e0 = jnp.exp(lbp_ref[0:1, :] - lmax)
    e1 = jnp.exp(lbp_ref[1:2, :] - lmax)
    lb = e0 / (e0 + e1)
    hf = proj(OFF_HF, HGRN_K)
    ef = jnp.exp(-jnp.abs(hf))
    rf = 1.0 / (1.0 + ef)
    sf = jnp.where(hf >= 0, rf, ef * rf)
    snf = jnp.where(hf >= 0, ef * rf, rf)
    f = lb + (1.0 - lb) * sf
    key = (1.0 - lb) * snf
    _gla_operands(_silu(proj(OFF_HQ, HGRN_K)) * (HGRN_DK ** -0.5), key, jnp.log(f),
                  hqt, hkt, hqh, hkh, hdec, tril)
    hv[...] = proj(OFF_HI, HGRN_V).astype(BF16)
    _recurrence(hqt, hkt, hqh, hkh, hv, hdec, sh, oh, HGRN_HEADS, HGRN_DK, HGRN_DV, tm, causal)

    for g in range(GLA_HEADS):
        cs = slice(g * GLA_DV, (g + 1) * GLA_DV)
        oa = og[:, cs]
        ya = (oa * lax.rsqrt(jnp.mean(oa * oa, axis=-1, keepdims=True) + EPS) * gnw_ref[...]
              * _silu(proj(OFF_GG + g * GLA_DV, GLA_DV)))
        yb_parts = []
        for j in range(GLA_DV // HGRN_DV):
            hs = slice(g * GLA_DV + j * HGRN_DV, g * GLA_DV + (j + 1) * HGRN_DV)
            ob = oh[:, hs]
            yb_parts.append(ob * lax.rsqrt(jnp.mean(ob * ob, axis=-1, keepdims=True) + EPS)
                            * hnw_ref[...])
        yb = jnp.concatenate(yb_parts, axis=1) * _silu(proj(OFF_HG + g * GLA_DV, GLA_DV))
        merged = (_sigmoid(proj(OFF_MA + g * GLA_DV, GLA_DV)) * ya
                  + _sigmoid(proj(OFF_MB + g * GLA_DV, GLA_DV)) * yb)
        mrg[:, cs] = merged.astype(BF16)

    o_ref[0] = x + gate_a * _dot(mrg[...], wout_ref[...])


def _const_spec(shape):
    nd = len(shape)
    return pl.BlockSpec(shape, lambda b, t: (0,) * nd, pipeline_mode=pl.Buffered(1))


def _mixer_call(x, ada, norm_w, w_in, w_gk2, b_gk2, gla_nw, lb_params, hgrn_nw, w_out):
    bsz, seq, _ = x.shape
    tm = TM_MIX
    act = pl.BlockSpec((1, tm, D_MODEL), lambda b, t: (b, t, 0))
    scratch = [
        pltpu.VMEM((tm, D_MODEL), BF16),
        pltpu.VMEM((tm, GLA_QK), BF16), pltpu.VMEM((tm, GLA_QK), BF16),
        pltpu.VMEM((tm, GLA_QK), BF16), pltpu.VMEM((tm, GLA_QK), BF16),
        pltpu.VMEM((tm, GLA_V), BF16), pltpu.VMEM((GLA_QK, tm), F32),
        pltpu.VMEM((tm, HGRN_K), BF16), pltpu.VMEM((tm, HGRN_K), BF16),
        pltpu.VMEM((tm, HGRN_K), BF16), pltpu.VMEM((tm, HGRN_K), BF16),
        pltpu.VMEM((tm, HGRN_V), BF16), pltpu.VMEM((HGRN_K, tm), F32),
        pltpu.VMEM((tm, GLA_V), F32), pltpu.VMEM((tm, HGRN_V), F32),
        pltpu.VMEM((tm, D_MODEL), BF16),
        pltpu.VMEM((GLA_HEADS, GLA_DK, GLA_DV), F32),
        pltpu.VMEM((HGRN_HEADS, HGRN_DK, HGRN_DV), F32),
    ]
    return pl.pallas_call(
        _mixer_kernel,
        grid=(bsz, seq // tm),
        in_specs=[
            act,
            pl.BlockSpec((1, 6, D_MODEL), lambda b, t: (b, 0, 0)),
            _const_spec(norm_w.shape),
            _const_spec(w_in.shape),
            _const_spec(w_gk2.shape),
            _const_spec(b_gk2.shape),
            _const_spec(gla_nw.shape),
            _const_spec(lb_params.shape),
            _const_spec(hgrn_nw.shape),
            _const_spec(w_out.shape),
        ],
        out_specs=act,
        out_shape=jax.ShapeDtypeStruct(x.shape, F32),
        scratch_shapes=scratch,
        compiler_params=pltpu.CompilerParams(
            dimension_semantics=("arbitrary", "arbitrary"),
            vmem_limit_bytes=VMEM_LIMIT_BYTES),
        name="mixer",
    )(x, ada, norm_w, w_in, w_gk2, b_gk2, gla_nw, lb_params, hgrn_nw, w_out)


def _ffn_kernel(x_ref, ada_ref, nw_ref, wup_ref, cw_ref, cb_ref, wdn_ref, fnw_ref, o_ref,
                h_scr, a_scr, c1_scr, c2_scr):
    @pl.when(pl.program_id(1) == 0)
    def _():
        c1_scr[...] = jnp.zeros_like(c1_scr)
        c2_scr[...] = jnp.zeros_like(c2_scr)

    x = x_ref[0]
    shift_f = ada_ref[0, 3:4, :]
    scale_f = ada_ref[0, 4:5, :]
    gate_f = ada_ref[0, 5:6, :]
    h_scr[...] = _modulated_rmsnorm(x, nw_ref[...], scale_f, shift_f).astype(BF16)

    row8 = lax.broadcasted_iota(jnp.int32, (SUBLANES, FFN_TILE), 0)

    def conv_tile(off):
        cs = slice(off, off + FFN_TILE)
        u = _dot(h_scr[...], wup_ref[:, cs])
        r1 = pltpu.roll(u, 1, 0)
        r2 = pltpu.roll(u, 2, 0)
        top1 = jnp.where(row8 < 1, c1_scr[:, cs], r1[:SUBLANES])
        top2 = jnp.where(row8 < 2, c2_scr[:, cs], r2[:SUBLANES])
        c1_scr[:, cs] = r1[:SUBLANES]
        c2_scr[:, cs] = r2[:SUBLANES]
        r1 = jnp.concatenate([top1, r1[SUBLANES:]], axis=0)
        r2 = jnp.concatenate([top2, r2[SUBLANES:]], axis=0)
        return (cw_ref[2:3, cs] * u + cw_ref[1:2, cs] * r1 + cw_ref[0:1, cs] * r2
                + cb_ref[:, cs])

    for j in range(D_FF // FFN_TILE):
        gate = conv_tile(j * FFN_TILE)
        up = conv_tile(D_FF + j * FFN_TILE)
        a_scr[:, j * FFN_TILE:(j + 1) * FFN_TILE] = (_silu(gate) * up).astype(BF16)

    h_res = x + gate_f * _dot(a_scr[...], wdn_ref[...])
    ms = jnp.mean(h_res * h_res, axis=-1, keepdims=True)
    o_ref[0] = h_res * lax.rsqrt(ms + EPS) * fnw_ref[...]


def _ffn_call(x, ada, norm_w, w_up, conv_w, conv_b, w_down, final_w):
    bsz, seq, _ = x.shape
    tm = TM_FFN
    act = pl.BlockSpec((1, tm, D_MODEL), lambda b, t: (b, t, 0))
    return pl.pallas_call(
        _ffn_kernel,
        grid=(bsz, seq // tm),
        in_specs=[
            act,
            pl.BlockSpec((1, 6, D_MODEL), lambda b, t: (b, 0, 0)),
            _const_spec(norm_w.shape),
            _const_spec(w_up.shape),
            _const_spec(conv_w.shape),
            _const_spec(conv_b.shape),
            _const_spec(w_down.shape),
            _const_spec(final_w.shape),
        ],
        out_specs=act,
        out_shape=jax.ShapeDtypeStruct(x.shape, F32),
        scratch_shapes=[
            pltpu.VMEM((tm, D_MODEL), BF16),
            pltpu.VMEM((tm, D_FF), BF16),
            pltpu.VMEM((SUBLANES, 2 * D_FF), F32),
            pltpu.VMEM((SUBLANES, 2 * D_FF), F32),
        ],
        compiler_params=pltpu.CompilerParams(
            dimension_semantics=("arbitrary", "arbitrary"),
            vmem_limit_bytes=VMEM_LIMIT_BYTES),
        name="ffn",
    )(x, ada, norm_w, w_up, conv_w, conv_b, w_down, final_w)


def kernel(x, c, w_ada, b_ada, norm_mix_w, w_in, w_gk2, b_gk2, gla_norm_w, hgrn_lb, hgrn_norm_w,
           w_out, norm_ffn_w, w_up, conv_w, conv_b, w_down, final_norm_w):
    bsz = x.shape[0]
    assert w_ada.shape[0] == 1, "single-layer block"
    assert x.shape[1] % TM_MIX == 0 and x.shape[1] % TM_FFN == 0

    assert hgrn_lb.shape[0] == 2
    c_pad = jnp.pad(c.astype(F32), ((0, -bsz % SUBLANES), (0, 0)))
    ada = _ada_call(c_pad, w_ada[0], b_ada[0][None, :])
    ada = jnp.transpose(ada[:, :bsz, :], (1, 0, 2))

    lr0 = OFF_GG + GLA_V
    w = w_in[0]
    w_in_r = jnp.concatenate(
        [w[:, :lr0], w[:, lr0 + GLA_LOWRANK:], w[:, lr0:lr0 + GLA_LOWRANK],
         jnp.zeros((D_MODEL, LANES - GLA_LOWRANK), w.dtype)], axis=1).astype(BF16)
    w_gk2_p = jnp.pad(w_gk2[0], ((0, LANES - GLA_LOWRANK), (0, 0))).astype(BF16)

    h_res = _mixer_call(x, ada, norm_mix_w[0][None, :], w_in_r, w_gk2_p, b_gk2[0][None, :],
                        gla_norm_w[0][None, :], hgrn_lb, hgrn_norm_w[0][None, :],
                        w_out[0].astype(BF16))
    return _ffn_call(h_res, ada, norm_ffn_w[0][None, :], w_up[0].astype(BF16), conv_w[0],
                     conv_b[0][None, :], w_down[0].astype(BF16), final_norm_w[None, :])
```

```python
import jax
import jax.numpy as jnp
from jax import lax
from jax.experimental import pallas as pl
from jax.experimental.pallas import tpu as pltpu

F32 = jnp.float32
BF16 = jnp.bfloat16

D_MODEL = 1024
GLA_HEADS = 4
GLA_QK = 512
GLA_V = 1024
GLA_DK = GLA_QK // GLA_HEADS
GLA_DV = GLA_V // GLA_HEADS
GLA_LOWRANK = 16
GLA_LOGIT_NORM = 16.0
HGRN_HEADS = 8
HGRN_K = 1024
HGRN_V = 1024
HGRN_DK = HGRN_K // HGRN_HEADS
HGRN_DV = HGRN_V // HGRN_HEADS
D_FF = 2816
CONV_W = 3
CHUNK = 64
EPS = 1e-6

LANES = 128
SUBLANES = 8
VMEM_LIMIT_BYTES = 56 * 1024 * 1024

A_GQ = 0
A_GK = A_GQ + GLA_QK
A_GV = A_GK + GLA_QK
A_GG = A_GV + GLA_V
A_END = A_GG + GLA_V
B_HQ = 0
B_HF = B_HQ + HGRN_K
B_HI = B_HF + HGRN_K
B_HG = B_HI + HGRN_V
B_MA = B_HG + HGRN_V
B_MB = B_MA + D_MODEL

TM_FFN = 512
FFN_TILE = 256


def _dot(a, b):
    return jnp.dot(a, b, preferred_element_type=F32)


def _sigmoid(x):
    return 0.5 * jnp.tanh(0.5 * x) + 0.5


def _silu(x):
    return x * _sigmoid(x)


def _log_sigmoid(x):
    return jnp.minimum(x, 0.0) - jnp.log1p(jnp.exp(-jnp.abs(x)))


def _rms_scale(x):
    return lax.rsqrt(jnp.mean(x * x, axis=-1, keepdims=True) + EPS)


def _ada_kernel(c_ref, w_ref, b_ref, o_ref):
    ca = _silu(c_ref[...])
    o_ref[0] = jnp.dot(ca, w_ref[...], precision=lax.Precision.HIGHEST,
                       preferred_element_type=F32) + b_ref[...]


def _ada_call(c_pad, w_ada, b_ada):
    n_tiles = w_ada.shape[1] // D_MODEL
    rows = c_pad.shape[0]
    return pl.pallas_call(
        _ada_kernel,
        grid=(n_tiles,),
        in_specs=[
            pl.BlockSpec((rows, D_MODEL), lambda j: (0, 0)),
            pl.BlockSpec((D_MODEL, D_MODEL), lambda j: (0, j)),
            pl.BlockSpec((1, D_MODEL), lambda j: (0, j)),
        ],
        out_specs=pl.BlockSpec((1, rows, D_MODEL), lambda j: (j, 0, 0)),
        out_shape=jax.ShapeDtypeStruct((n_tiles, rows, D_MODEL), F32),
        name="ada",
    )(c_pad, w_ada, b_ada)


def _chunk_cumsum(tril, x, nb):
    hi = x.astype(BF16)
    lo = (x - hi.astype(F32)).astype(BF16)
    parts = []
    for i in range(nb):
        rows = slice(i * CHUNK, (i + 1) * CHUNK)
        parts.append(_dot(tril, hi[rows]) + _dot(tril, lo[rows]))
    return jnp.concatenate(parts, axis=0)


def _gla_operands(q, k, log_g, qt_ref, kt_ref, qh_ref, kh_ref, dec_ref, tril, nb):
    rows, c = q.shape
    b = _chunk_cumsum(tril, log_g, nb)
    b3 = b.reshape(nb, CHUNK, c)
    b_ref = b3[:, CHUNK // 2 - 1:CHUNK // 2, :]
    b_last = b3[:, CHUNK - 1:CHUNK, :]
    q3 = q.reshape(nb, CHUNK, c)
    k3 = k.reshape(nb, CHUNK, c)
    qt_ref[...] = (q3 * jnp.exp(b3 - b_ref)).reshape(rows, c).astype(BF16)
    kt_ref[...] = (k3 * jnp.exp(b_ref - b3)).reshape(rows, c).astype(BF16)
    qh_ref[...] = (q3 * jnp.exp(b3)).reshape(rows, c).astype(BF16)
    kh_ref[...] = (k3 * jnp.exp(b_last - b3)).reshape(rows, c).astype(BF16)
    dec_ref[...] = jnp.exp(jnp.transpose(b))


def _recurrence(qt_ref, kt_ref, qh_ref, kh_ref, v_ref, dec_ref, s_ref, o_ref,
                heads, dk, dv, nb, causal):
    chains = [(i, h) for i in range(nb) for h in range(heads)]

    def sl(i, h):
        return (slice(i * CHUNK, (i + 1) * CHUNK), slice(h * dk, (h + 1) * dk),
                slice(h * dv, (h + 1) * dv))

    probs = []
    for i, h in chains:
        rows, ks, _ = sl(i, h)
        scores = lax.dot_general(qt_ref[rows, ks], kt_ref[rows, ks],
                                 (((1,), (1,)), ((), ())),
                                 preferred_element_type=F32)
        probs.append(jnp.where(causal, scores, 0.0).astype(BF16))
    for (i, h), p in zip(chains, probs):
        rows, ks, vs = sl(i, h)
        lhs = jnp.concatenate([qh_ref[rows, ks], p], axis=1)
        rhs = jnp.concatenate([s_ref[i * heads + h].astype(BF16), v_ref[rows, vs]], axis=0)
        o_ref[rows, vs] = _dot(lhs, rhs)
    for i, h in chains:
        rows, ks, vs = sl(i, h)
        upd = lax.dot_general(kh_ref[rows, ks], v_ref[rows, vs],
                              (((0,), (0,)), ((), ())),
                              preferred_element_type=F32)
        col = i * CHUNK + CHUNK - 1
        s_ref[i * heads + h] = dec_ref[ks, col:col + 1] * s_ref[i * heads + h] + upd


def _mixer_kernel(x_ref, ada_ref, nw_ref, wa_ref, wlr_ref, wb_ref, wgk2_ref, bgk2_ref, gnw_ref,
                  lbp_ref, hnw_ref, wout_ref, o_ref,
                  h_scr, gqt, gkt, gqh, gkh, gv, gdec, hqt, hkt, hqh, hkh, hv, hdec,
                  og, oh, mrg, sg, sh):
    nb = x_ref.shape[0]
    rows = nb * CHUNK

    @pl.when(pl.program_id(0) == 0)
    def _():
        sg[...] = jnp.zeros_like(sg)
        sh[...] = jnp.zeros_like(sh)

    x = x_ref[...]
    shift_a = ada_ref[:, 0:1, :]
    scale_a = ada_ref[:, 1:2, :]
    gate_a = ada_ref[:, 2:3, :]
    h = x * _rms_scale(x) * nw_ref[...] * (1.0 + scale_a) + shift_a
    h_scr[...] = h.reshape(rows, D_MODEL).astype(BF16)

    def proj_a(off, width):
        return _dot(h_scr[...], wa_ref[:, off:off + width])

    def proj_b(off, width):
        return _dot(h_scr[...], wb_ref[:, off:off + width])

    crow = lax.broadcasted_iota(jnp.int32, (CHUNK, CHUNK), 0)
    ccol = lax.broadcasted_iota(jnp.int32, (CHUNK, CHUNK), 1)
    causal = crow >= ccol
    tril = causal.astype(BF16)

    glr = _dot(h_scr[...], wlr_ref[...]).astype(BF16)
    z = _dot(glr, wgk2_ref[...]) + bgk2_ref[...]
    log_alpha = _log_sigmoid(z) * (1.0 / GLA_LOGIT_NORM)
    _gla_operands(proj_a(A_GQ, GLA_QK) * (GLA_DK ** -0.5), proj_a(A_GK, GLA_QK), log_alpha,
                  gqt, gkt, gqh, gkh, gdec, tril, nb)
    gv[...] = proj_a(A_GV, GLA_V).astype(BF16)
    _recurrence(gqt, gkt, gqh, gkh, gv, gdec, sg, og, GLA_HEADS, GLA_DK, GLA_DV, nb, causal)

    lmax = jnp.maximum(lbp_ref[0:1, :], lbp_ref[1:2, :])
    e0 = jnp.exp(lbp_ref[0:1, :] - lmax)
    e1 = jnp.exp(lbp_ref[1:2, :] - lmax)
    lb = e0 / (e0 + e1)
    th = jnp.tanh(0.5 * proj_b(B_HF, HGRN_K))
    sf = 0.5 * th + 0.5
    snf = 0.5 - 0.5 * th
    f = lb + (1.0 - lb) * sf
    key = (1.0 - lb) * snf
    _gla_operands(_silu(proj_b(B_HQ, HGRN_K)) * (HGRN_DK ** -0.5), key, jnp.log(f),
                  hqt, hkt, hqh, hkh, hdec, tril, nb)
    hv[...] = proj_b(B_HI, HGRN_V).astype(BF16)
    _recurrence(hqt, hkt, hqh, hkh, hv, hdec, sh, oh, HGRN_HEADS, HGRN_DK, HGRN_DV, nb, causal)

    for g in range(GLA_HEADS):
        cs = slice(g * GLA_DV, (g + 1) * GLA_DV)
        oa = og[:, cs]
        ya = oa * _rms_scale(oa) * gnw_ref[...] * _silu(proj_a(A_GG + g * GLA_DV, GLA_DV))
        yb_parts = []
        for j in range(GLA_DV // HGRN_DV):
            hs = slice(g * GLA_DV + j * HGRN_DV, g * GLA_DV + (j + 1) * HGRN_DV)
            ob = oh[:, hs]
            yb_parts.append(ob * _rms_scale(ob) * hnw_ref[...])
        yb = jnp.concatenate(yb_parts, axis=1) * _silu(proj_b(B_HG + g * GLA_DV, GLA_DV))
        merged = (_sigmoid(proj_b(B_MA + g * GLA_DV, GLA_DV)) * ya
                  + _sigmoid(proj_b(B_MB + g * GLA_DV, GLA_DV)) * yb)
        mrg[:, cs] = merged.astype(BF16)

    mix = _dot(mrg[...], wout_ref[...]).reshape(nb, CHUNK, D_MODEL)
    o_ref[...] = x + gate_a * mix


def _const_spec(shape):
    nd = len(shape)
    return pl.BlockSpec(shape, lambda *_: (0,) * nd, pipeline_mode=pl.Buffered(1))


def _mixer_call(x, ada, norm_w, w_a, w_lr, w_b, w_gk2, b_gk2, gla_nw, lb_params, hgrn_nw, w_out):
    bsz, seq, _ = x.shape
    rows = bsz * CHUNK
    act = pl.BlockSpec((bsz, CHUNK, D_MODEL), lambda t: (0, t, 0))
    consts = (norm_w, w_a, w_lr, w_b, w_gk2, b_gk2, gla_nw, lb_params, hgrn_nw, w_out)
    scratch = [
        pltpu.VMEM((rows, D_MODEL), BF16),
        pltpu.VMEM((rows, GLA_QK), BF16), pltpu.VMEM((rows, GLA_QK), BF16),
        pltpu.VMEM((rows, GLA_QK), BF16), pltpu.VMEM((rows, GLA_QK), BF16),
        pltpu.VMEM((rows, GLA_V), BF16), pltpu.VMEM((GLA_QK, rows), F32),
        pltpu.VMEM((rows, HGRN_K), BF16), pltpu.VMEM((rows, HGRN_K), BF16),
        pltpu.VMEM((rows, HGRN_K), BF16), pltpu.VMEM((rows, HGRN_K), BF16),
        pltpu.VMEM((rows, HGRN_V), BF16), pltpu.VMEM((HGRN_K, rows), F32),
        pltpu.VMEM((rows, GLA_V), F32), pltpu.VMEM((rows, HGRN_V), F32),
        pltpu.VMEM((rows, D_MODEL), BF16),
        pltpu.VMEM((bsz * GLA_HEADS, GLA_DK, GLA_DV), F32),
        pltpu.VMEM((bsz * HGRN_HEADS, HGRN_DK, HGRN_DV), F32),
    ]
    return pl.pallas_call(
        _mixer_kernel,
        grid=(seq // CHUNK,),
        in_specs=[act, pl.BlockSpec((bsz, 6, D_MODEL), lambda t: (0, 0, 0))]
        + [_const_spec(a.shape) for a in consts],
        out_specs=act,
        out_shape=jax.ShapeDtypeStruct(x.shape, F32),
        scratch_shapes=scratch,
        compiler_params=pltpu.CompilerParams(
            dimension_semantics=("arbitrary",),
            vmem_limit_bytes=VMEM_LIMIT_BYTES),
        name="mixer",
    )(x, ada, *consts)


def _ffn_kernel(x_ref, ada_ref, nw_ref, wup_ref, cw_ref, cb_ref, wdn_ref, fnw_ref, o_ref,
                h_scr, a_scr, c1_scr, c2_scr):
    @pl.when(pl.program_id(1) == 0)
    def _():
        c1_scr[...] = jnp.zeros_like(c1_scr)
        c2_scr[...] = jnp.zeros_like(c2_scr)

    x = x_ref[0]
    shift_f = ada_ref[0, 3:4, :]
    scale_f = ada_ref[0, 4:5, :]
    gate_f = ada_ref[0, 5:6, :]
    h_scr[...] = (x * _rms_scale(x) * nw_ref[...] * (1.0 + scale_f) + shift_f).astype(BF16)

    row8 = lax.broadcasted_iota(jnp.int32, (SUBLANES, FFN_TILE), 0)

    def conv_tile(off):
        cs = slice(off, off + FFN_TILE)
        u = _dot(h_scr[...], wup_ref[:, cs])
        r1 = pltpu.roll(u, 1, 0)
        r2 = pltpu.roll(u, 2, 0)
        top1 = jnp.where(row8 < 1, c1_scr[:, cs], r1[:SUBLANES])
        top2 = jnp.where(row8 < 2, c2_scr[:, cs], r2[:SUBLANES])
        c1_scr[:, cs] = r1[:SUBLANES]
        c2_scr[:, cs] = r2[:SUBLANES]
        r1 = jnp.concatenate([top1, r1[SUBLANES:]], axis=0)
        r2 = jnp.concatenate([top2, r2[SUBLANES:]], axis=0)
        return (cw_ref[2:3, cs] * u + cw_ref[1:2, cs] * r1 + cw_ref[0:1, cs] * r2
                + cb_ref[:, cs])

    for j in range(D_FF // FFN_TILE):
        gate = conv_tile(j * FFN_TILE)
        up = conv_tile(D_FF + j * FFN_TILE)
        a_scr[:, j * FFN_TILE:(j + 1) * FFN_TILE] = (_silu(gate) * up).astype(BF16)

    h_res = x + gate_f * _dot(a_scr[...], wdn_ref[...])
    o_ref[0] = h_res * _rms_scale(h_res) * fnw_ref[...]


def _ffn_call(x, ada, norm_w, w_up, conv_w, conv_b, w_down, final_w):
    bsz, seq, _ = x.shape
    tm = TM_FFN
    act = pl.BlockSpec((1, tm, D_MODEL), lambda b, t: (b, t, 0))
    consts = (norm_w, w_up, conv_w, conv_b, w_down, final_w)
    return pl.pallas_call(
        _ffn_kernel,
        grid=(bsz, seq // tm),
        in_specs=[act, pl.BlockSpec((1, 6, D_MODEL), lambda b, t: (b, 0, 0))]
        + [_const_spec(a.shape) for a in consts],
        out_specs=act,
        out_shape=jax.ShapeDtypeStruct(x.shape, F32),
        scratch_shapes=[
            pltpu.VMEM((tm, D_MODEL), BF16),
            pltpu.VMEM((tm, D_FF), BF16),
            pltpu.VMEM((SUBLANES, 2 * D_FF), F32),
            pltpu.VMEM((SUBLANES, 2 * D_FF), F32),
        ],
        compiler_params=pltpu.CompilerParams(
            dimension_semantics=("arbitrary", "arbitrary"),
            vmem_limit_bytes=VMEM_LIMIT_BYTES),
        name="ffn",
    )(x, ada, *consts)


def kernel(x, c, w_ada, b_ada, norm_mix_w, w_in, w_gk2, b_gk2, gla_norm_w, hgrn_lb, hgrn_norm_w,
           w_out, norm_ffn_w, w_up, conv_w, conv_b, w_down, final_norm_w):
    bsz, seq, _ = x.shape
    assert w_ada.shape[0] == 1, "single-layer block"
    assert hgrn_lb.shape[0] == 2
    assert seq % CHUNK == 0 and seq % TM_FFN == 0

    c_pad = jnp.pad(c.astype(F32), ((0, -bsz % SUBLANES), (0, 0)))
    ada = _ada_call(c_pad, w_ada[0], b_ada[0][None, :])
    ada = jnp.transpose(ada[:, :bsz, :], (1, 0, 2))

    w = w_in[0]
    w_a = w[:, :A_END].astype(BF16)
    w_lr = jnp.pad(w[:, A_END:A_END + GLA_LOWRANK], ((0, 0), (0, LANES - GLA_LOWRANK))).astype(BF16)
    w_b = w[:, A_END + GLA_LOWRANK:].astype(BF16)
    w_gk2_p = jnp.pad(w_gk2[0], ((0, LANES - GLA_LOWRANK), (0, 0))).astype(BF16)

    h_res = _mixer_call(x, ada, norm_mix_w[0][None, :], w_a, w_lr, w_b, w_gk2_p, b_gk2[0][None, :],
                        gla_norm_w[0][None, :], hgrn_lb, hgrn_norm_w[0][None, :],
                        w_out[0].astype(BF16))
    return _ffn_call(h_res, ada, norm_ffn_w[0][None, :], w_up[0].astype(BF16), conv_w[0],
                     conv_b[0][None, :], w_down[0].astype(BF16), final_norm_w[None, :])
```
